```python
import math
import jax
import jax.numpy as jnp
from jax import lax
import numpy as np

D_MODEL = 2048
BATCH = 1
SEQ = 8192
DEPTH = 2

MLA_HEADS = 8
Q_LORA = 512
KV_LORA = 512
NOPE_DIM = 128
ROPE_DIM = 64
V_DIM = 128
ROPE_THETA = 10000.0
DIL_HEADS = 8
DIL_HEAD_DIM = 128
DIL_BRANCHES = ((128, 1), (512, 4), (2048, 16))
DIL_WIDTH = DIL_HEADS * DIL_HEAD_DIM
BLK = 128
IN_COLS = Q_LORA + KV_LORA + ROPE_DIM + 3 * DIL_WIDTH
MIX_WIDTH = MLA_HEADS * V_DIM + DIL_WIDTH
S5_GROUP = 16
S5_GROUPS = D_MODEL // S5_GROUP
S5_STATE = 64
FFN_HIDDEN = -(-8 * D_MODEL // (3 * 256)) * 256
PLE_DIM = 256
DEEPNORM_ALPHA = (2.0 * DEPTH) ** 0.25
DEEPNORM_BETA = (8.0 * DEPTH) ** -0.25
NEG = -1e30

kernel_name = 'hybrid_mla_dilated_s5_block'


def _layernorm(x, g, b, eps=1e-5):
    xf = x.astype(jnp.float32)
    mu = xf.mean(-1, keepdims=True)
    var = jnp.square(xf - mu).mean(-1, keepdims=True)
    y = (xf - mu) * lax.rsqrt(var + eps) * g.astype(jnp.float32) + b.astype(jnp.float32)
    return y.astype(x.dtype)


def _rmsnorm(x, g, eps=1e-6):
    xf = x.astype(jnp.float32)
    y = xf * lax.rsqrt(jnp.square(xf).mean(-1, keepdims=True) + eps) * g.astype(jnp.float32)
    return y.astype(x.dtype)


def _rope(t, positions):
    half = t.shape[-1] // 2
    inv_freq = ROPE_THETA ** (-jnp.arange(half, dtype=jnp.float32) / half)
    ang = positions.astype(jnp.float32)[..., None] * inv_freq
    cos, sin = jnp.cos(ang)[:, :, None, :], jnp.sin(ang)[:, :, None, :]
    t1, t2 = t[..., :half], t[..., half:]
    return jnp.concatenate([t1 * cos - t2 * sin, t1 * sin + t2 * cos], axis=-1).astype(t.dtype)


def _alibi_slopes(n):
    return jnp.asarray(2.0 ** (-8.0 * np.arange(1, n + 1) / n), dtype=jnp.float32)


def _mla_causal(q, k, v):
    B, S, H, Dqk = q.shape
    nb = S // BLK
    scale = Dqk ** -0.5
    qb = q.reshape(B, nb, BLK, H, Dqk).transpose(1, 0, 2, 3, 4)
    kpos = jnp.arange(S)

    def block(args):
        qblk, bi = args
        s = jnp.einsum('bqhd,bkhd->bhqk', qblk, k).astype(jnp.float32) * scale
        qpos = bi * BLK + jnp.arange(BLK)
        s = jnp.where(kpos[None, :] <= qpos[:, None], s, NEG)
        pr = jax.nn.softmax(s, axis=-1).astype(v.dtype)
        return jnp.einsum('bhqk,bkhd->bqhd', pr, v)

    out = lax.map(block, (qb, jnp.arange(nb)))
    return out.transpose(1, 0, 2, 3, 4).reshape(B, S, H * v.shape[-1])


def _dilated_branch(q, k, v, window, dilation, slopes):
    B, S, H, Dh = q.shape
    d = dilation
    L = S // d
    steps = window // d
    nb = -(-L // BLK)
    Lp = nb * BLK

    def to_classes(t):
        t = t.reshape(B, L, d, H, Dh).transpose(0, 2, 1, 3, 4).reshape(B * d, L, H, Dh)
        return jnp.pad(t, ((0, 0), (0, Lp - L), (0, 0), (0, 0)))

    def windows(t):
        tb = jnp.pad(t, ((0, 0), (BLK, 0), (0, 0), (0, 0))).reshape(B * d, nb + 1, BLK, H, Dh)
        return jnp.concatenate([tb[:, :-1], tb[:, 1:]], axis=2)

    qb = to_classes(q).reshape(B * d, nb, BLK, H, Dh)
    kw = windows(to_classes(k))
    vw = windows(to_classes(v))
    s = jnp.einsum('zbqhd,zbkhd->zbhqk', qb, kw).astype(jnp.float32) * (Dh ** -0.5)
    qi = jnp.arange(BLK)[:, None]
    ki = jnp.arange(2 * BLK)[None, :]
    dist = qi + BLK - ki
    kabs = jnp.arange(nb)[:, None, None] * BLK + ki[None] - BLK
    valid = (dist >= 0)[None] & (dist <= steps)[None] & (kabs >= 0)
    bias = -slopes[:, None, None] * (d * dist).astype(jnp.float32)[None]
    s = jnp.where(valid[None, :, None], s + bias[None, None], NEG)
    m = s.max(-1, keepdims=True)
    e = jnp.exp(s - m)
    l = e.sum(-1, keepdims=True)
    o = jnp.einsum('zbhqk,zbkhd->zbqhd', (e / l).astype(v.dtype), vw)
    lse = (m + jnp.log(l))[..., 0]
    o = o.reshape(B * d, Lp, H, Dh)[:, :L].reshape(B, d, L, H, Dh).transpose(0, 2, 1, 3, 4).reshape(B, S, H, Dh)
    lse = lse.transpose(0, 1, 3, 2).reshape(B * d, Lp, H)[:, :L].reshape(B, d, L, H).transpose(0, 2, 1, 3).reshape(B, S, H)
    return o, lse


def _hybrid_attention(h, positions, w_in, q_norm, w_q_b, kv_norm, w_kv_b, w_out):
    B, S, _ = h.shape
    splits = list(np.cumsum([Q_LORA, KV_LORA, ROPE_DIM, DIL_WIDTH, DIL_WIDTH]))
    q_lat, kv_lat, k_pe, qd, kd, vd = jnp.split(h @ w_in, splits, axis=-1)
    q = (_rmsnorm(q_lat, q_norm) @ w_q_b).reshape(B, S, MLA_HEADS, NOPE_DIM + ROPE_DIM)
    q = jnp.concatenate([q[..., :NOPE_DIM], _rope(q[..., NOPE_DIM:], positions)], axis=-1)
    kv = (_rmsnorm(kv_lat, kv_norm) @ w_kv_b).reshape(B, S, MLA_HEADS, NOPE_DIM + V_DIM)
    k_pe = jnp.broadcast_to(_rope(k_pe[:, :, None, :], positions), (B, S, MLA_HEADS, ROPE_DIM))
    k = jnp.concatenate([kv[..., :NOPE_DIM], k_pe], axis=-1)
    out_a = _mla_causal(q, k, kv[..., NOPE_DIM:])
    qd = qd.reshape(B, S, DIL_HEADS, DIL_HEAD_DIM)
    kd = kd.reshape(B, S, DIL_HEADS, DIL_HEAD_DIM)
    vd = vd.reshape(B, S, DIL_HEADS, DIL_HEAD_DIM)
    slopes = _alibi_slopes(DIL_HEADS)
    outs, lses = [], []
    for window, dilation in DIL_BRANCHES:
        o_g, lse_g = _dilated_branch(qd, kd, vd, window, dilation, slopes)
        outs.append(o_g)
        lses.append(lse_g)
    wts = jax.nn.softmax(jnp.stack(lses), axis=0)
    out_b = jnp.einsum('gbsh,gbshd->bshd', wts, jnp.stack(outs).astype(jnp.float32))
    out_b = out_b.astype(h.dtype).reshape(B, S, DIL_WIDTH)
    return jnp.concatenate([out_a, out_b], axis=-1) @ w_out


def _s5_glu(h, a_re, a_im, log_dt, b_re, b_im, c_re, c_im, d_skip, w_glu):
    B, S, D = h.shape
    f32 = jnp.float32
    u = h.astype(f32).reshape(B, S, S5_GROUPS, S5_GROUP)
    A = lax.complex(a_re.astype(f32), a_im.astype(f32))
    dt = jnp.exp(log_dt.astype(f32))[:, None]
    A_bar = jnp.exp(A * dt)
    B_bar = ((A_bar - 1.0) / A)[..., None] * lax.complex(b_re.astype(f32), b_im.astype(f32))
    C_mat = lax.complex(c_re.astype(f32), c_im.astype(f32))
    bu = jnp.einsum('bsgc,gpc->bsgp', u.astype(jnp.complex64), B_bar)
    a = jnp.broadcast_to(A_bar, bu.shape)

    def combine(left, right):
        a_l, b_l = left
        a_r, b_r = right
        return a_r * a_l, a_r * b_l + b_r

    _, states = lax.associative_scan(combine, (a, bu), axis=1)
    y = jnp.einsum('bsgp,gcp->bsgc', states, C_mat).real.reshape(B, S, D)
    y = y + d_skip.astype(f32) * h.astype(f32)
    z = jax.nn.gelu(y).astype(h.dtype)
    val, gate = jnp.split(z @ w_glu, 2, axis=-1)
    return val * jax.nn.sigmoid(gate)


def _swiglu(h, w_in, w_out):
    g, u = jnp.split(h @ w_in, 2, axis=-1)
    return (jax.nn.silu(g) * u) @ w_out


def setup_inputs(seed: int = 0) -> dict:
    key = jax.random.key(seed)
    ks = iter(jax.random.split(key, 40))
    ne, no = (DEPTH + 1) // 2, DEPTH // 2
    f32 = jnp.float32

    def nrm(shape, scale):
        return jax.random.normal(next(ks), shape, f32) * scale

    x = nrm((BATCH, SEQ, D_MODEL), 1.0)
    p = nrm((DEPTH, BATCH, SEQ, PLE_DIM), 1.0)
    positions = jnp.broadcast_to(jnp.arange(SEQ, dtype=jnp.int32), (BATCH, SEQ))
    attn_w_in = nrm((ne, D_MODEL, IN_COLS), D_MODEL ** -0.5)
    mla_q_norm = 1.0 + nrm((ne, Q_LORA), 0.02)
    mla_w_q_b = nrm((ne, Q_LORA, MLA_HEADS * (NOPE_DIM + ROPE_DIM)), Q_LORA ** -0.5)
    mla_kv_norm = 1.0 + nrm((ne, KV_LORA), 0.02)
    mla_w_kv_b = nrm((ne, KV_LORA, MLA_HEADS * (NOPE_DIM + V_DIM)), KV_LORA ** -0.5)
    attn_w_out = nrm((ne, MIX_WIDTH, D_MODEL), MIX_WIDTH ** -0.5 * DEEPNORM_BETA)
    s5_a_re = -0.5 + nrm((no, S5_GROUPS, S5_STATE), 0.01)
    s5_a_im = math.pi * jnp.arange(S5_STATE, dtype=f32) + nrm((no, S5_GROUPS, S5_STATE), 0.01)
    s5_log_dt = jax.random.uniform(next(ks), (no, S5_GROUPS), f32, math.log(1e-3), math.log(1e-1))
    s5_b_re = nrm((no, S5_GROUPS, S5_STATE, S5_GROUP), (2 * S5_GROUP) ** -0.5)
    s5_b_im = nrm((no, S5_GROUPS, S5_STATE, S5_GROUP), (2 * S5_GROUP) ** -0.5)
    s5_c_re = nrm((no, S5_GROUPS, S5_GROUP, S5_STATE), (2 * S5_STATE) ** -0.5)
    s5_c_im = nrm((no, S5_GROUPS, S5_GROUP, S5_STATE), (2 * S5_STATE) ** -0.5)
    s5_d = nrm((no, D_MODEL), 1.0)
    s5_w_glu = jnp.concatenate([nrm((no, D_MODEL, D_MODEL), D_MODEL ** -0.5 * DEEPNORM_BETA),
                                nrm((no, D_MODEL, D_MODEL), D_MODEL ** -0.5)], axis=-1)
    ln1_g = 1.0 + nrm((DEPTH, D_MODEL), 0.02)
    ln1_b = nrm((DEPTH, D_MODEL), 0.02)
    ffn_w_in = nrm((DEPTH, D_MODEL, 2 * FFN_HIDDEN), D_MODEL ** -0.5)
    ffn_w_out = nrm((DEPTH, FFN_HIDDEN, D_MODEL), FFN_HIDDEN ** -0.5 * DEEPNORM_BETA)
    ple_w = nrm((DEPTH, PLE_DIM, D_MODEL), PLE_DIM ** -0.5)
    ple_gate_w = nrm((DEPTH, D_MODEL, D_MODEL), D_MODEL ** -0.5)
    ln2_g = 1.0 + nrm((DEPTH, D_MODEL), 0.02)
    ln2_b = nrm((DEPTH, D_MODEL), 0.02)
    return {'x': x, 'p': p, 'positions': positions,
            'attn_w_in': attn_w_in, 'mla_q_norm': mla_q_norm, 'mla_w_q_b': mla_w_q_b,
            'mla_kv_norm': mla_kv_norm, 'mla_w_kv_b': mla_w_kv_b, 'attn_w_out': attn_w_out,
            's5_a_re': s5_a_re, 's5_a_im': s5_a_im, 's5_log_dt': s5_log_dt,
            's5_b_re': s5_b_re, 's5_b_im': s5_b_im, 's5_c_re': s5_c_re, 's5_c_im': s5_c_im,
            's5_d': s5_d, 's5_w_glu': s5_w_glu,
            'ln1_g': ln1_g, 'ln1_b': ln1_b, 'ffn_w_in': ffn_w_in, 'ffn_w_out': ffn_w_out,
            'ple_w': ple_w, 'ple_gate_w': ple_gate_w, 'ln2_g': ln2_g, 'ln2_b': ln2_b}


def reference(x, p, positions, attn_w_in, mla_q_norm, mla_w_q_b, mla_kv_norm, mla_w_kv_b, attn_w_out,
              s5_a_re, s5_a_im, s5_log_dt, s5_b_re, s5_b_im, s5_c_re, s5_c_im, s5_d, s5_w_glu,
              ln1_g, ln1_b, ffn_w_in, ffn_w_out, ple_w, ple_gate_w, ln2_g, ln2_b):
    h = x
    for i in range(DEPTH):
        j = i // 2
        if i % 2 == 0:
            mix = _hybrid_attention(h, positions, attn_w_in[j], mla_q_norm[j], mla_w_q_b[j],
                                    mla_kv_norm[j], mla_w_kv_b[j], attn_w_out[j])
        else:
            mix = _s5_glu(h, s5_a_re[j], s5_a_im[j], s5_log_dt[j], s5_b_re[j], s5_b_im[j],
                          s5_c_re[j], s5_c_im[j], s5_d[j], s5_w_glu[j])
        h = _layernorm(DEEPNORM_ALPHA * h + mix, ln1_g[i], ln1_b[i])
        ple = (p[i] @ ple_w[i]) * jax.nn.sigmoid(h @ ple_gate_w[i])
        h = _layernorm(DEEPNORM_ALPHA * h + _swiglu(h, ffn_w_in[i], ffn_w_out[i]) + ple, ln2_g[i], ln2_b[i])
    return h
```

```python
import functools
import math

import numpy as np
import jax
import jax.numpy as jnp
from jax import lax
from jax.experimental import pallas as pl
from jax.experimental.pallas import tpu as pltpu

F32 = jnp.float32
BF16 = jnp.bfloat16

D_MODEL = 2048
DEPTH = 2
MLA_HEADS = 8
Q_LORA = 512
KV_LORA = 512
NOPE_DIM = 128
ROPE_DIM = 64
V_DIM = 128
ROPE_THETA = 10000.0
QK_PAD = 256
DIL_HEADS = 8
DIL_HEAD_DIM = 128
DIL_DILATIONS = (1, 4, 16)
DIL_STEPS = 128
DIL_WIDTH = DIL_HEADS * DIL_HEAD_DIM
DIL_SPAN = DIL_STEPS * max(DIL_DILATIONS)
S5_GROUP = 16
S5_GROUPS = D_MODEL // S5_GROUP
S5_STATE = 64
S5_CHUNK = 16
S5_PAIRS = S5_GROUPS // 2
FFN_HIDDEN = -(-8 * D_MODEL // (3 * 256)) * 256
PLE_DIM = 256
ALPHA = (2.0 * DEPTH) ** 0.25
NEG = -1e30

VMEM_LIMIT = 60 * 1024 * 1024


def _params(*sem):
    return pltpu.CompilerParams(dimension_semantics=sem, vmem_limit_bytes=VMEM_LIMIT)


def _resident(shape):
    nd = len(shape)
    return pl.BlockSpec(shape, lambda *_: (0,) * nd, pipeline_mode=pl.Buffered(1))


def _dot(a, b):
    return jnp.dot(a, b, preferred_element_type=F32)


def _dot_nt(a, b):
    return lax.dot_general(a, b, (((1,), (1,)), ((), ())), preferred_element_type=F32)


def _layernorm(y, g, b):
    mu = jnp.mean(y, axis=-1, keepdims=True)
    yc = y - mu
    var = jnp.mean(yc * yc, axis=-1, keepdims=True)
    return yc * lax.rsqrt(var + 1e-5) * g + b


def _rmsnorm(x, g):
    return x * lax.rsqrt(jnp.mean(x * x, axis=-1, keepdims=True) + 1e-6) * g


def _mm_kernel(a_ref, w_ref, o_ref, *, tn):
    a = a_ref[...].astype(BF16)
    for n in range(w_ref.shape[1] // tn):
        o_ref[:, n * tn:(n + 1) * tn] = _dot(a, w_ref[:, n * tn:(n + 1) * tn]).astype(o_ref.dtype)


def _matmul(a, w, out_dtype, tm=512, tn=512):
    m, k = a.shape
    n = w.shape[1]
    return pl.pallas_call(
        functools.partial(_mm_kernel, tn=tn),
        grid=(m // tm,),
        in_specs=[pl.BlockSpec((tm, k), lambda i: (i, 0)), _resident((k, n))],
        out_specs=pl.BlockSpec((tm, n), lambda i: (i, 0)),
        out_shape=jax.ShapeDtypeStruct((m, n), out_dtype),
        compiler_params=_params("parallel"),
        name="dil_in_proj",
    )(a, w)


def _mla_prep_kernel(x_ref, pos_ref, wa_ref, qn_ref, kvn_ref, wq_ref, wkv_ref, invf_ref,
                     q_out, k_out, v_out):
    tm = x_ref.shape[0]
    lat = _dot(x_ref[...].astype(BF16), wa_ref[...])
    q_lat = _rmsnorm(lat[:, :Q_LORA], qn_ref[...])
    kv_lat = _rmsnorm(lat[:, Q_LORA:Q_LORA + KV_LORA], kvn_ref[...])
    kpe = lat[:, Q_LORA + KV_LORA:]
    q = _dot(q_lat.astype(BF16), wq_ref[...])
    kv = _dot(kv_lat.astype(BF16), wkv_ref[...])

    ang = pos_ref[...].astype(F32) * invf_ref[...]
    cos = jnp.cos(ang)
    sin = jnp.sin(ang)
    lane = lax.broadcasted_iota(jnp.int32, (tm, 128), 1)
    first_half = lane < ROPE_DIM // 2
    sin_signed = jnp.where(first_half, -sin, sin)

    def rope(t):
        swapped = jnp.where(first_half, pltpu.roll(t, 128 - ROPE_DIM // 2, 1), pltpu.roll(t, ROPE_DIM // 2, 1))
        return t * cos + swapped * sin_signed

    scale = (NOPE_DIM + ROPE_DIM) ** -0.5
    kpe_r = rope(kpe).astype(BF16)
    for h in range(MLA_HEADS):
        c0 = h * QK_PAD
        q_out[:, c0:c0 + 128] = (q[:, c0:c0 + 128] * scale).astype(BF16)
        q_out[:, c0 + 128:c0 + 256] = (rope(q[:, c0 + 128:c0 + 256]) * scale).astype(BF16)
        k_out[:, c0:c0 + 128] = kv[:, c0:c0 + 128].astype(BF16)
        k_out[:, c0 + 128:c0 + 256] = kpe_r
        v_out[:, h * V_DIM:(h + 1) * V_DIM] = kv[:, c0 + 128:c0 + 256].astype(BF16)


def _mla_prep(x, pos, wa, qn, kvn, wq, wkv, invf, tm=512):
    s = x.shape[0]
    row = lambda w: pl.BlockSpec((tm, w), lambda i: (i, 0))
    return pl.pallas_call(
        _mla_prep_kernel,
        grid=(s // tm,),
        in_specs=[row(D_MODEL), row(1), _resident(wa.shape), _resident(qn.shape), _resident(kvn.shape),
                  _resident(wq.shape), _resident(wkv.shape), _resident(invf.shape)],
        out_specs=[row(MLA_HEADS * QK_PAD), row(MLA_HEADS * QK_PAD), row(MLA_HEADS * V_DIM)],
        out_shape=[jax.ShapeDtypeStruct((s, MLA_HEADS * QK_PAD), BF16),
                   jax.ShapeDtypeStruct((s, MLA_HEADS * QK_PAD), BF16),
                   jax.ShapeDtypeStruct((s, MLA_HEADS * V_DIM), BF16)],
        compiler_params=_params("parallel"),
        name="mla_prep",
    )(x, pos, wa, qn, kvn, wq, wkv, invf)


def _mla_flash_kernel(q_ref, k_ref, v_ref, o_ref, m_sc, l_sc, acc_sc, *, blk):
    i = pl.program_id(0)
    j = pl.program_id(1)

    @pl.when(j == 0)
    def _():
        m_sc[...] = jnp.full(m_sc.shape, NEG, F32)
        l_sc[...] = jnp.zeros(l_sc.shape, F32)
        acc_sc[...] = jnp.zeros(acc_sc.shape, F32)

    def step(diagonal):
        if diagonal:
            row = lax.broadcasted_iota(jnp.int32, (blk, blk), 0)
            col = lax.broadcasted_iota(jnp.int32, (blk, blk), 1)
            causal = col <= row
        for h in range(MLA_HEADS):
            q = q_ref[:, h * QK_PAD:(h + 1) * QK_PAD]
            k = k_ref[:, h * QK_PAD:(h + 1) * QK_PAD]
            v = v_ref[:, h * V_DIM:(h + 1) * V_DIM]
            s = _dot_nt(q, k)
            if diagonal:
                s = jnp.where(causal, s, NEG)
            m_prev = m_sc[h]
            m_new = jnp.maximum(m_prev, jnp.max(s, axis=-1, keepdims=True))
            p = jnp.exp(s - m_new)
            a = jnp.exp(m_prev - m_new)
            l_sc[h] = a * l_sc[h] + jnp.sum(p, axis=-1, keepdims=True)
            acc_sc[h] = a * acc_sc[h] + _dot(p.astype(BF16), v)
            m_sc[h] = m_new

    @pl.when(j < i)
    def _():
        step(False)

    @pl.when(j == i)
    def _():
        step(True)
        for h in range(MLA_HEADS):
            o_ref[:, h * V_DIM:(h + 1) * V_DIM] = (acc_sc[h] / l_sc[h]).astype(o_ref.dtype)


def _mla_flash(q, k, v, blk=512):
    s = q.shape[0]
    nb = s // blk
    return pl.pallas_call(
        functools.partial(_mla_flash_kernel, blk=blk),
        grid=(nb, nb),
        in_specs=[pl.BlockSpec((blk, MLA_HEADS * QK_PAD), lambda i, j: (i, 0)),
                  pl.BlockSpec((blk, MLA_HEADS * QK_PAD), lambda i, j: (jnp.minimum(i, j), 0)),
                  pl.BlockSpec((blk, MLA_HEADS * V_DIM), lambda i, j: (jnp.minimum(i, j), 0))],
        out_specs=pl.BlockSpec((blk, MLA_HEADS * V_DIM), lambda i, j: (i, 0)),
        out_shape=jax.ShapeDtypeStruct((s, MLA_HEADS * V_DIM), BF16),
        scratch_shapes=[pltpu.VMEM((MLA_HEADS, blk, 1), F32), pltpu.VMEM((MLA_HEADS, blk, 1), F32),
                        pltpu.VMEM((MLA_HEADS, blk, V_DIM), F32)],
        compiler_params=_params("parallel", "arbitrary"),
        name="mla_flash",
    )(q, k, v)


def _dil_kernel(slope_ref, q_ref, kc_ref, kp_ref, vc_ref, vp_ref, o_ref, kbuf, vbuf, o_sc, m_sc, l_sc):
    h = pl.program_id(0)
    j = pl.program_id(1)
    span = DIL_SPAN
    blk = DIL_STEPS
    kbuf[0:span] = kp_ref[...]
    kbuf[span:] = kc_ref[...]
    vbuf[0:span] = vp_ref[...]
    vbuf[span:] = vc_ref[...]
    slope = slope_ref[h]
    qi = lax.broadcasted_iota(jnp.int32, (blk, 2 * blk), 0)
    ki = lax.broadcasted_iota(jnp.int32, (blk, 2 * blk), 1)
    dist = qi + blk - ki
    in_window = jnp.logical_and(dist >= 0, dist <= DIL_STEPS)
    in_window_first = jnp.logical_and(in_window, ki >= jnp.where(j > 0, 0, blk))
    distf = dist.astype(F32)
    scale = DIL_HEAD_DIM ** -0.5
    for g, d in enumerate(DIL_DILATIONS):
        bias = distf * (-(slope * d))
        for c in range(span // (blk * d)):
            valid = in_window_first if c == 0 else in_window
            for r in range(d):
                q0 = c * blk * d + r
                k0 = span + (c - 1) * blk * d + r
                if d == 1:
                    q = q_ref[q0:q0 + blk, :]
                    k = kbuf[k0:k0 + 2 * blk, :]
                    v = vbuf[k0:k0 + 2 * blk, :]
                else:
                    q = q_ref[pl.ds(q0, blk, stride=d), :]
                    k = kbuf[pl.ds(k0, 2 * blk, stride=d), :]
                    v = vbuf[pl.ds(k0, 2 * blk, stride=d), :]
                s = _dot_nt(q.astype(BF16), k.astype(BF16)) * scale + bias
                s = jnp.where(valid, s, NEG)
                m = jnp.max(s, axis=-1, keepdims=True)
                p = jnp.exp(s - m)
                l = jnp.sum(p, axis=-1, keepdims=True)
                o = _dot(p.astype(BF16), v.astype(BF16))
                mb = jnp.broadcast_to(m, (blk, DIL_HEAD_DIM))
                lb = jnp.broadcast_to(l, (blk, DIL_HEAD_DIM))
                if d == 1:
                    o_sc[g, q0:q0 + blk, :] = o
                    m_sc[g, q0:q0 + blk, :] = mb
                    l_sc[g, q0:q0 + blk, :] = lb
                else:
                    o_sc[g, pl.ds(q0, blk, stride=d), :] = o
                    m_sc[g, pl.ds(q0, blk, stride=d), :] = mb
                    l_sc[g, pl.ds(q0, blk, stride=d), :] = lb
    m_all = jnp.maximum(jnp.maximum(m_sc[0], m_sc[1]), m_sc[2])
    num = jnp.zeros((span, DIL_HEAD_DIM), F32)
    den = jnp.zeros((span, DIL_HEAD_DIM), F32)
    for g in range(len(DIL_DILATIONS)):
        w = jnp.exp(m_sc[g] - m_all)
        num = num + w * o_sc[g]
        den = den + w * l_sc[g]
    o_ref[...] = (num / den).astype(o_ref.dtype)


def _dilated(qkv, slopes):
    s = qkv.shape[0]
    span = DIL_SPAN
    hd = DIL_HEAD_DIM
    nh = DIL_HEADS
    cur = lambda off: pl.BlockSpec((span, hd), lambda h, j: (j, off + h))
    prev = lambda off: pl.BlockSpec((span, hd), lambda h, j: (jnp.maximum(j - 1, 0), off + h))
    return pl.pallas_call(
        _dil_kernel,
        grid=(nh, s // span),
        in_specs=[pl.BlockSpec(memory_space=pltpu.SMEM),
                  cur(0), cur(nh), prev(nh), cur(2 * nh), prev(2 * nh)],
        out_specs=pl.BlockSpec((span, hd), lambda h, j: (j, h)),
        out_shape=jax.ShapeDtypeStruct((s, nh * hd), BF16),
        scratch_shapes=[pltpu.VMEM((2 * span, hd), F32), pltpu.VMEM((2 * span, hd), F32),
                        pltpu.VMEM((3, span, hd), F32), pltpu.VMEM((3, span, hd), F32),
                        pltpu.VMEM((3, span, hd), F32)],
        compiler_params=_params("parallel", "arbitrary"),
        name="dilated_attn",
    )(slopes, qkv, qkv, qkv, qkv, qkv)


def _ple_base(h, h16, p_ref, plew_ref, gatew_ref, base_ref, tn):
    p16 = p_ref[...].astype(BF16)
    for n in range(D_MODEL // tn):
        cs = slice(n * tn, (n + 1) * tn)
        emb = _dot(p16, plew_ref[:, cs])
        gate = jax.nn.sigmoid(_dot(h16, gatew_ref[:, cs]))
        base_ref[:, cs] = ALPHA * h[:, cs] + emb * gate


def _attn_out_kernel(a1_ref, a2_ref, w_ref, x_ref, g_ref, b_ref, p_ref, plew_ref, gatew_ref,
                     h16_ref, base_ref, *, tn):
    half = a1_ref.shape[1]
    mix = _dot(a1_ref[...], w_ref[0:half, :]) + _dot(a2_ref[...], w_ref[half:, :])
    h = _layernorm(ALPHA * x_ref[...] + mix, g_ref[...], b_ref[...])
    h16 = h.astype(BF16)
    h16_ref[...] = h16
    _ple_base(h, h16, p_ref, plew_ref, gatew_ref, base_ref, tn)


def _attn_out(a1, a2, w, x, g, b, p, plew, gatew, tm=512, tn=512):
    s = x.shape[0]
    row = lambda wd: pl.BlockSpec((tm, wd), lambda i: (i, 0))
    return pl.pallas_call(
        functools.partial(_attn_out_kernel, tn=tn),
        grid=(s // tm,),
        in_specs=[row(a1.shape[1]), row(a2.shape[1]), _resident(w.shape), row(D_MODEL),
                  _resident(g.shape), _resident(b.shape), row(PLE_DIM), _resident(plew.shape),
                  _resident(gatew.shape)],
        out_specs=[row(D_MODEL), row(D_MODEL)],
        out_shape=[jax.ShapeDtypeStruct((s, D_MODEL), BF16), jax.ShapeDtypeStruct((s, D_MODEL), F32)],
        compiler_params=_params("parallel"),
        name="attn_out_ln_ple",
    )(a1, a2, w, x, g, b, p, plew, gatew)


def _glu_out_kernel(z_ref, w_ref, x_ref, g_ref, b_ref, p_ref, plew_ref, gatew_ref,
                    h16_ref, base_ref, y_sc, *, tn):
    z = z_ref[...]
    for n in range(D_MODEL // tn):
        cs = slice(n * tn, (n + 1) * tn)
        val = _dot(z, w_ref[:, cs])
        gate = _dot(z, w_ref[:, D_MODEL + n * tn:D_MODEL + (n + 1) * tn])
        y_sc[:, cs] = ALPHA * x_ref[:, cs] + val * jax.nn.sigmoid(gate)
    h = _layernorm(y_sc[...], g_ref[...], b_ref[...])
    h16 = h.astype(BF16)
    h16_ref[...] = h16
    _ple_base(h, h16, p_ref, plew_ref, gatew_ref, base_ref, tn)


def _glu_out(z, w, x, g, b, p, plew, gatew, tm=512, tn=512):
    s = x.shape[0]
    row = lambda wd: pl.BlockSpec((tm, wd), lambda i: (i, 0))
    return pl.pallas_call(
        functools.partial(_glu_out_kernel, tn=tn),
        grid=(s // tm,),
        in_specs=[row(D_MODEL), _resident(w.shape), row(D_MODEL), _resident(g.shape), _resident(b.shape),
                  row(PLE_DIM), _resident(plew.shape), _resident(gatew.shape)],
        out_specs=[row(D_MODEL), row(D_MODEL)],
        out_shape=[jax.ShapeDtypeStruct((s, D_MODEL), BF16), jax.ShapeDtypeStruct((s, D_MODEL), F32)],
        scratch_shapes=[pltpu.VMEM((tm, D_MODEL), F32)],
        compiler_params=_params("parallel"),
        name="glu_out_ln_ple",
    )(z, w, x, g, b, p, plew, gatew)


def _ffn_kernel(h16_ref, wg_ref, wu_ref, wo_ref, base_ref, g_ref, b_ref, o_ref, acc_sc):
    j = pl.program_id(1)

    @pl.when(j == 0)
    def _():
        acc_sc[...] = base_ref[...]

    h16 = h16_ref[...]
    gate = _dot(h16, wg_ref[...])
    up = _dot(h16, wu_ref[...])
    act = (gate * jax.nn.sigmoid(gate) * up).astype(BF16)
    acc_sc[...] += _dot(act, wo_ref[...])

    @pl.when(j == pl.num_programs(1) - 1)
    def _():
        o_ref[...] = _layernorm(acc_sc[...], g_ref[...], b_ref[...])


def _ffn(h16, w_in, w_out, base, g, b, tm=512, th=512):
    s = h16.shape[0]
    nh = FFN_HIDDEN // th
    row = pl.BlockSpec((tm, D_MODEL), lambda i, j: (i, 0))
    return pl.pallas_call(
        _ffn_kernel,
        grid=(s // tm, nh),
        in_specs=[row,
                  pl.BlockSpec((D_MODEL, th), lambda i, j: (0, j)),
                  pl.BlockSpec((D_MODEL, th), lambda i, j: (0, nh + j)),
                  pl.BlockSpec((th, D_MODEL), lambda i, j: (j, 0)),
                  row, _resident(g.shape), _resident(b.shape)],
        out_specs=row,
        out_shape=jax.ShapeDtypeStruct((s, D_MODEL), F32),
        scratch_shapes=[pltpu.VMEM((tm, D_MODEL), F32)],
        compiler_params=_params("parallel", "arbitrary"),
        name="ffn_ln",
    )(h16, w_in, w_in, w_out, base, g, b)


def _gelu_tanh(y):
    return 0.5 * y * (1.0 + jnp.tanh(math.sqrt(2.0 / math.pi) * (y + 0.044715 * (y * y * y))))


def _s5_kernel(u_ref, toep_ref, bre_ref, bim_ref, cre_ref, cim_ref, a16_ref, d_ref, z_ref,
               vre_sc, vim_sc, xre_sc, xim_sc):
    npair, nchunk, _ = u_ref.shape
    w = S5_CHUNK * S5_GROUP
    for pr in range(npair):
        u0 = u_ref[pr, :, 0:w].astype(BF16)
        u1 = u_ref[pr, :, w:2 * w].astype(BF16)
        vre_sc[pr] = _dot(u0, bre_ref[pr, 0]) + _dot(u1, bre_ref[pr, 1])
        vim_sc[pr] = _dot(u0, bim_ref[pr, 0]) + _dot(u1, bim_ref[pr, 1])

    a_re = a16_ref[:, 0]
    a_im = a16_ref[:, 1]

    def body(c8, carry):
        xr, xi = carry
        base = pl.multiple_of(c8 * 8, 8)
        vr = vre_sc[:, pl.ds(base, 8), :]
        vi = vim_sc[:, pl.ds(base, 8), :]
        prev_r, prev_i = [], []
        for t in range(8):
            prev_r.append(xr)
            prev_i.append(xi)
            nr = a_re * xr - a_im * xi + vr[:, t:t + 1, :]
            ni = a_re * xi + a_im * xr + vi[:, t:t + 1, :]
            xr, xi = nr, ni
        xre_sc[:, pl.ds(base, 8), :] = jnp.concatenate(prev_r, axis=1)
        xim_sc[:, pl.ds(base, 8), :] = jnp.concatenate(prev_i, axis=1)
        return xr, xi

    zero = jnp.zeros((npair, 1, 2 * S5_STATE), F32)
    lax.fori_loop(0, nchunk // 8, body, (zero, zero))

    for pr in range(npair):
        xr16 = xre_sc[pr].astype(BF16)
        xi16 = xim_sc[pr].astype(BF16)
        for gi in range(2):
            u = u_ref[pr, :, gi * w:(gi + 1) * w]
            y = (_dot(u.astype(BF16), toep_ref[pr, gi]) + _dot(xr16, cre_ref[pr, gi])
                 + _dot(xi16, cim_ref[pr, gi]) + d_ref[pr, :, gi * w:(gi + 1) * w] * u)
            z_ref[pr, :, gi * w:(gi + 1) * w] = _gelu_tanh(y).astype(z_ref.dtype)


def _s5(u, toep, bre, bim, cre, cim, a16, dskip, npair=4):
    pairs, nchunk, width = u.shape
    blk3 = lambda shp: pl.BlockSpec((npair,) + shp, lambda i: (i,) + (0,) * len(shp))
    return pl.pallas_call(
        _s5_kernel,
        grid=(pairs // npair,),
        in_specs=[blk3((nchunk, width)), blk3(toep.shape[1:]), blk3(bre.shape[1:]), blk3(bim.shape[1:]),
                  blk3(cre.shape[1:]), blk3(cim.shape[1:]), blk3(a16.shape[1:]), blk3(dskip.shape[1:])],
        out_specs=blk3((nchunk, width)),
        out_shape=jax.ShapeDtypeStruct(u.shape, BF16),
        scratch_shapes=[pltpu.VMEM((npair, nchunk, 2 * S5_STATE), F32) for _ in range(4)],
        compiler_params=_params("parallel"),
        name="s5_scan",
    )(u, toep, bre, bim, cre, cim, a16, dskip)


def _s5_tables(a_re, a_im, log_dt, b_re, b_im, c_re, c_im, d_skip):
    g, p, t = S5_GROUPS, S5_STATE, S5_CHUNK
    hi = lax.Precision.HIGHEST
    dt = jnp.exp(log_dt)[:, None]
    lam_re, lam_im = a_re * dt, a_im * dt
    kk = jnp.arange(t + 1, dtype=F32)[:, None, None]
    mag = jnp.exp(kk * lam_re)
    pw_re, pw_im = mag * jnp.cos(kk * lam_im), mag * jnp.sin(kk * lam_im)
    nr, ni = pw_re[1] - 1.0, pw_im[1]
    den = a_re * a_re + a_im * a_im
    f_re, f_im = (nr * a_re + ni * a_im) / den, (ni * a_re - nr * a_im) / den
    bb_re = f_re[..., None] * b_re - f_im[..., None] * b_im
    bb_im = f_re[..., None] * b_im + f_im[..., None] * b_re
    rv_re, rv_im = pw_re[t - 1::-1][:t], pw_im[t - 1::-1][:t]
    bm_re = rv_re[:, :, :, None] * bb_re[None] - rv_im[:, :, :, None] * bb_im[None]
    bm_im = rv_re[:, :, :, None] * bb_im[None] + rv_im[:, :, :, None] * bb_re[None]
    bm_re = bm_re.transpose(1, 0, 3, 2).reshape(g, t * S5_GROUP, p)
    bm_im = bm_im.transpose(1, 0, 3, 2).reshape(g, t * S5_GROUP, p)
    cp_re = c_re[None] * pw_re[:, :, None, :] - c_im[None] * pw_im[:, :, None, :]
    cp_im = c_re[None] * pw_im[:, :, None, :] + c_im[None] * pw_re[:, :, None, :]
    cm_re = cp_re[1:].transpose(1, 3, 0, 2).reshape(g, p, t * S5_GROUP)
    cm_im = -cp_im[1:].transpose(1, 3, 0, 2).reshape(g, p, t * S5_GROUP)
    kern = (jnp.einsum("lgcp,gpd->glcd", cp_re[:t], bb_re, precision=hi)
            - jnp.einsum("lgcp,gpd->glcd", cp_im[:t], bb_im, precision=hi))
    lag = np.arange(t)[None, :] - np.arange(t)[:, None]
    toep = kern[:, np.clip(lag, 0, None)]
    toep = jnp.where((lag >= 0)[None, :, :, None, None], toep, 0.0)
    toep = toep.transpose(0, 1, 4, 2, 3).reshape(g, t * S5_GROUP, t * S5_GROUP)
    even = (np.arange(g) % 2 == 0)[:, None, None]
    zb = jnp.zeros_like(bm_re)
    zc = jnp.zeros_like(cm_re)
    pack_b = lambda m: jnp.where(even, jnp.concatenate([m, zb], -1), jnp.concatenate([zb, m], -1))
    pack_c = lambda m: jnp.where(even, jnp.concatenate([m, zc], 1), jnp.concatenate([zc, m], 1))
    pair = lambda m: m.reshape((S5_PAIRS, 2) + m.shape[1:])
    a16 = jnp.stack([pw_re[t].reshape(S5_PAIRS, 1, 2 * p), pw_im[t].reshape(S5_PAIRS, 1, 2 * p)], axis=1)
    dsk = jnp.broadcast_to(d_skip.reshape(S5_PAIRS, 1, 2, 1, S5_GROUP), (S5_PAIRS, 1, 2, t, S5_GROUP))
    return (pair(toep.astype(BF16)), pair(pack_b(bm_re).astype(BF16)), pair(pack_b(bm_im).astype(BF16)),
            pair(pack_c(cm_re).astype(BF16)), pair(pack_c(cm_im).astype(BF16)), a16,
            dsk.reshape(S5_PAIRS, 1, 2 * t * S5_GROUP))


def kernel(x, p, positions, attn_w_in, mla_q_norm, mla_w_q_b, mla_kv_norm, mla_w_kv_b, attn_w_out,
           s5_a_re, s5_a_im, s5_log_dt, s5_b_re, s5_b_im, s5_c_re, s5_c_im, s5_d, s5_w_glu,
           ln1_g, ln1_b, ffn_w_in, ffn_w_out, ple_w, ple_gate_w, ln2_g, ln2_b):
    s = x.shape[1]
    x2 = x.reshape(s, D_MODEL)
    row = lambda v: v.reshape(1, -1)

    n_lat = Q_LORA + KV_LORA + ROPE_DIM
    w_in = attn_w_in[0]
    wa = jnp.pad(w_in[:, :n_lat], ((0, 0), (0, 128 - ROPE_DIM))).astype(BF16)
    wd = w_in[:, n_lat:].astype(BF16)
    qk_dim = NOPE_DIM + ROPE_DIM
    wq = jnp.pad(mla_w_q_b[0].reshape(Q_LORA, MLA_HEADS, qk_dim),
                 ((0, 0), (0, 0), (0, QK_PAD - qk_dim))).reshape(Q_LORA, MLA_HEADS * QK_PAD).astype(BF16)
    wkv = mla_w_kv_b[0].astype(BF16)
    half = ROPE_DIM // 2
    inv_freq = ROPE_THETA ** (-np.arange(half, dtype=np.float32) / half)
    invf = jnp.asarray(np.concatenate([inv_freq, inv_freq, np.zeros(128 - ROPE_DIM, np.float32)])[None, :])
    pos = positions.reshape(s, 1)
    q, k, v = _mla_prep(x2, pos, wa, row(mla_q_norm[0]), row(mla_kv_norm[0]), wq, wkv, invf)
    out_a = _mla_flash(q, k, v)
    qkv_d = _matmul(x2, wd, F32)
    slopes = jnp.asarray(2.0 ** (-8.0 * np.arange(1, DIL_HEADS + 1) / DIL_HEADS), dtype=F32)
    out_b = _dilated(qkv_d, slopes)
    h16, base = _attn_out(out_a, out_b, attn_w_out[0].astype(BF16), x2, row(ln1_g[0]), row(ln1_b[0]),
                          p[0, 0], ple_w[0].astype(BF16), ple_gate_w[0].astype(BF16))
    h = _ffn(h16, ffn_w_in[0].astype(BF16), ffn_w_out[0].astype(BF16), base, row(ln2_g[0]), row(ln2_b[0]))

    nchunk = s // S5_CHUNK
    tables = _s5_tables(s5_a_re[0], s5_a_im[0], s5_log_dt[0], s5_b_re[0], s5_b_im[0], s5_c_re[0],
                        s5_c_im[0], s5_d[0])
    u = h.reshape(nchunk, S5_CHUNK, S5_PAIRS, 2, S5_GROUP).transpose(2, 0, 3, 1, 4)
    u = u.reshape(S5_PAIRS, nchunk, 2 * S5_CHUNK * S5_GROUP)
    z = _s5(u, *tables)
    z = z.reshape(S5_PAIRS, nchunk, 2, S5_CHUNK, S5_GROUP).transpose(1, 3, 0, 2, 4).reshape(s, D_MODEL)
    h16, base = _glu_out(z, s5_w_glu[0].astype(BF16), h, row(ln1_g[1]), row(ln1_b[1]),
                         p[1, 0], ple_w[1].astype(BF16), ple_gate_w[1].astype(BF16))
    h = _ffn(h16, ffn_w_in[1].astype(BF16), ffn_w_out[1].astype(BF16), base, row(ln2_g[1]), row(ln2_b[1]))
    return h.reshape(x.shape)
```
